```python
import math
import jax, jax.numpy as jnp
from jax import lax
import numpy as np

D_MODEL = 1024
BATCH = 16
SEQ = 4096
DEPTH = 1
DEC_BATCH = 128
DEC_SEQ = 8
PAST_LEN = 8192
PAGE_SIZE = 128

D_MIX = D_MODEL
ATT_HEADS = 8
HEAD_DIM = 64
ATT_WIDTH = ATT_HEADS * HEAD_DIM
KV_HEADS = 2
HPG = ATT_HEADS // KV_HEADS
KV_W = KV_HEADS * HEAD_DIM
GATE_W = 3 * ATT_HEADS
CMP_BLOCK = 32
CMP_STRIDE = 16
CMP_HIDDEN = 128
SEL_BLOCK = 64
N_SEL = 16
WINDOW = 512
SEL_QB = 64
WIN_QB = 128
SSM_WIDTH = D_MIX - ATT_WIDTH
SSM_HEADDIM = 64
SSM_HEADS = SSM_WIDTH // SSM_HEADDIM
SSM_GROUPS = 2
D_STATE = 128
CONV_W = 4
CONV_DIM = SSM_WIDTH + 2 * SSM_GROUPS * D_STATE
SSD_CHUNK = 128
D_FF = 2816
EPS = 1e-6
NEG = -1e30
FORCE = 1e9
SPLITS = (ATT_WIDTH, KV_W, KV_W, KV_W, KV_W, KV_W, KV_W, GATE_W, SSM_WIDTH, CONV_DIM, SSM_HEADS)
IN_W = sum(SPLITS)

kernel_name = 'nsa_mamba2_macaron_hybrid_step'


def rmsnorm(x, w):
    xf = x.astype(jnp.float32)
    y = xf * lax.rsqrt(jnp.mean(xf * xf, axis=-1, keepdims=True) + EPS)
    return (y * w.astype(jnp.float32)).astype(x.dtype)


def ffn_half(x, nw, wg, wu, wd):
    h = rmsnorm(x, nw)
    return x + 0.5 * ((jax.nn.silu(h @ wg) * (h @ wu)) @ wd)


def alibi_slopes():
    s = 2.0 ** (-8.0 * np.arange(1, ATT_HEADS + 1) / ATT_HEADS)
    return jnp.asarray(s, dtype=jnp.float32).reshape(KV_HEADS, HPG)


def masked_softmax(s, mask):
    s = jnp.where(mask, s.astype(jnp.float32), NEG)
    e = jnp.where(mask, jnp.exp(s - jnp.max(s, axis=-1, keepdims=True)), 0.0)
    return e / jnp.maximum(jnp.sum(e, axis=-1, keepdims=True), 1e-30)


def split_proj(u):
    cuts = [int(c) for c in np.cumsum(SPLITS)[:-1]]
    return jnp.split(u, cuts, axis=-1)


def compress(rows, pe, w1, w2):
    b, T = rows.shape[:2]
    nc = T // CMP_STRIDE
    ch = rows[:, :nc * CMP_STRIDE].reshape(b, nc, CMP_STRIDE, KV_HEADS, HEAD_DIM)
    blk = jnp.concatenate([ch[:, :-1], ch[:, 1:]], axis=2) + pe[None, None, :, None, :]
    blk = blk.transpose(0, 1, 3, 2, 4).reshape(b, nc - 1, KV_HEADS, CMP_BLOCK * HEAD_DIM)
    return jax.nn.silu(blk @ w1) @ w2


def cmp_attention(q, q_pos, kc, vc, slopes):
    b, Tq = q.shape[:2]
    nb = kc.shape[1]
    qg = q.reshape(b, Tq, KV_HEADS, HPG, HEAD_DIM)
    s = jnp.einsum('btghd,bngd->bghtn', qg, kc).astype(jnp.float32) / math.sqrt(HEAD_DIM)
    end = jnp.arange(nb) * CMP_STRIDE + CMP_BLOCK - 1
    dist = (q_pos[:, None] - end[None, :]).astype(jnp.float32)
    s = s - slopes[None, :, :, None, None] * dist
    p = masked_softmax(s, dist >= 0)
    o = jnp.einsum('bghtn,bngd->btghd', p.astype(vc.dtype), vc)
    return o.reshape(b, Tq, ATT_HEADS, HEAD_DIM), p


def select_blocks(p, q_pos, n_sel_blocks):
    nb = p.shape[-1]
    i = jnp.arange(nb)[:, None]
    j = jnp.arange(n_sel_blocks)[None, :]
    ratio = SEL_BLOCK // CMP_STRIDE
    overlap = ((i // ratio) == j).astype(jnp.float32) + (((i + 1) // ratio) == j).astype(jnp.float32)
    imp = jnp.einsum('bghtn,nj->bgtj', p, overlap)
    own = (q_pos // SEL_BLOCK)[:, None]
    forced = (j == own) | (j == 0)
    valid = j * SEL_BLOCK <= q_pos[:, None]
    imp = jnp.where(forced, FORCE, jnp.where(valid, imp, -1.0))
    _, idx = lax.top_k(imp, min(N_SEL, n_sel_blocks))
    return idx


def sel_chunk(q, q_pos, idx, kb, vb, slopes):
    b, Q = q.shape[:2]
    k = idx.shape[-1]
    bi = jnp.arange(b)[:, None, None, None]
    gi = jnp.arange(KV_HEADS)[None, :, None, None]
    kg = kb[bi, gi, idx]
    vg = vb[bi, gi, idx]
    qg = q.reshape(b, Q, KV_HEADS, HPG, HEAD_DIM)
    s = jnp.einsum('bqghd,bgqksd->bghqks', qg, kg).astype(jnp.float32) / math.sqrt(HEAD_DIM)
    pos = idx[..., None] * SEL_BLOCK + jnp.arange(SEL_BLOCK)
    dist = (q_pos[None, None, :, None, None] - pos).astype(jnp.float32)[:, :, None]
    s = s - slopes[None, :, :, None, None, None] * dist
    s = s.reshape(b, KV_HEADS, HPG, Q, k * SEL_BLOCK)
    mask = (dist >= 0).reshape(b, KV_HEADS, 1, Q, k * SEL_BLOCK)
    p = masked_softmax(s, mask)
    o = jnp.einsum('bghqn,bgqnd->bqghd', p.astype(vg.dtype), vg.reshape(b, KV_HEADS, Q, k * SEL_BLOCK, HEAD_DIM))
    return o.reshape(b, Q, ATT_HEADS, HEAD_DIM)


def sel_attention(q, q_pos, idx, k_rows, v_rows, slopes, qb):
    b, T = q.shape[:2]
    Tk = k_rows.shape[1]
    nsel = -(-Tk // SEL_BLOCK)
    pad = nsel * SEL_BLOCK - Tk

    def blocks(r):
        r = jnp.pad(r, ((0, 0), (0, pad), (0, 0), (0, 0)))
        return r.reshape(b, nsel, SEL_BLOCK, KV_HEADS, HEAD_DIM).transpose(0, 3, 1, 2, 4)

    kb, vb = blocks(k_rows), blocks(v_rows)
    nq = T // qb
    k = idx.shape[-1]
    qs = q.reshape(b, nq, qb, ATT_HEADS, HEAD_DIM).transpose(1, 0, 2, 3, 4)
    ps = q_pos.reshape(nq, qb)
    ids = idx.reshape(b, KV_HEADS, nq, qb, k).transpose(2, 0, 1, 3, 4)
    o = lax.map(lambda a: sel_chunk(a[0], a[1], a[2], kb, vb, slopes), (qs, ps, ids))
    return o.transpose(1, 0, 2, 3, 4).reshape(b, T, ATT_HEADS, HEAD_DIM)


def window_chunk(q, q_pos, k, v, k_pos, slopes):
    b, Q = q.shape[:2]
    qg = q.reshape(b, Q, KV_HEADS, HPG, HEAD_DIM)
    s = jnp.einsum('bqghd,bkgd->bghqk', qg, k).astype(jnp.float32) / math.sqrt(HEAD_DIM)
    dist = q_pos[:, None] - k_pos[None, :]
    mask = (dist >= 0) & (dist < WINDOW) & (k_pos >= 0)[None, :]
    s = s - slopes[None, :, :, None, None] * dist.astype(jnp.float32)
    p = masked_softmax(s, mask)
    o = jnp.einsum('bghqk,bkgd->bqghd', p.astype(v.dtype), v)
    return o.reshape(b, Q, ATT_HEADS, HEAD_DIM)


def window_attention_prompt(q, k, v, slopes):
    b, T = q.shape[:2]
    nq = T // WIN_QB
    kp = jnp.pad(k, ((0, 0), (WINDOW, 0), (0, 0), (0, 0)))
    vp = jnp.pad(v, ((0, 0), (WINDOW, 0), (0, 0), (0, 0)))

    def body(i):
        qs = lax.dynamic_slice_in_dim(q, i * WIN_QB, WIN_QB, axis=1)
        ks = lax.dynamic_slice_in_dim(kp, i * WIN_QB, WINDOW + WIN_QB, axis=1)
        vs = lax.dynamic_slice_in_dim(vp, i * WIN_QB, WINDOW + WIN_QB, axis=1)
        qpos = i * WIN_QB + jnp.arange(WIN_QB)
        kpos = i * WIN_QB - WINDOW + jnp.arange(WINDOW + WIN_QB)
        return window_chunk(qs, qpos, ks, vs, kpos, slopes)

    o = lax.map(body, jnp.arange(nq))
    return o.transpose(1, 0, 2, 3, 4).reshape(b, T, ATT_HEADS, HEAD_DIM)


def causal_conv(xbc, conv_state, w, bias):
    T = xbc.shape[1]
    xp = jnp.concatenate([conv_state.astype(xbc.dtype), xbc], axis=1)
    out = bias
    for tap in range(CONV_W):
        out = out + xp[:, tap:tap + T] * w[tap]
    return jax.nn.silu(out), xp[:, xp.shape[1] - (CONV_W - 1):]


def ssd(x, dt, A, Bm, Cm, h0):
    b, T = x.shape[:2]
    L = SSD_CHUNK if T % SSD_CHUNK == 0 else T
    nc = T // L
    rep = SSM_HEADS // SSM_GROUPS
    f32 = jnp.float32
    xc = x.astype(f32).reshape(b, nc, L, SSM_HEADS, SSM_HEADDIM)
    dtc = dt.reshape(b, nc, L, SSM_HEADS)
    Bh = jnp.repeat(Bm.astype(f32), rep, axis=2).reshape(b, nc, L, SSM_HEADS, D_STATE)
    Ch = jnp.repeat(Cm.astype(f32), rep, axis=2).reshape(b, nc, L, SSM_HEADS, D_STATE)
    acs = jnp.cumsum(dtc * A, axis=2)
    causal = jnp.tril(jnp.ones((L, L), dtype=bool))[:, :, None]
    seg = acs[:, :, :, None, :] - acs[:, :, None, :, :]
    decay = jnp.where(causal, jnp.exp(jnp.where(causal, seg, 0.0)), 0.0)
    xdt = xc * dtc[..., None]
    cb = jnp.einsum('bclhn,bcshn->bclsh', Ch, Bh)
    y_diag = jnp.einsum('bclsh,bcshp->bclhp', cb * decay, xdt)
    decay_end = jnp.exp(acs[:, :, -1:, :] - acs)
    states = jnp.einsum('bclhn,bclh,bclhp->bchpn', Bh, decay_end, xdt)
    chunk_decay = jnp.exp(acs[:, :, -1, :])

    def step(h, inp):
        st, cd = inp
        return h * cd[:, :, None, None] + st, h

    hT, h_prev = lax.scan(step, h0.astype(f32), (states.transpose(1, 0, 2, 3, 4), chunk_decay.transpose(1, 0, 2)))
    h_prev = h_prev.transpose(1, 0, 2, 3, 4)
    y_off = jnp.einsum('bclhn,bchpn,bclh->bclhp', Ch, h_prev, jnp.exp(acs))
    return (y_diag + y_off).reshape(b, T, SSM_HEADS, SSM_HEADDIM), hT


def mamba_mixer(z, xbc, dtr, conv0, h0, prm):
    b, T = z.shape[:2]
    f32 = jnp.float32
    xbc, conv_new = causal_conv(xbc, conv0, prm['conv_w'], prm['conv_b'])
    xs, bm, cm = jnp.split(xbc, [SSM_WIDTH, SSM_WIDTH + SSM_GROUPS * D_STATE], axis=-1)
    xs = xs.reshape(b, T, SSM_HEADS, SSM_HEADDIM)
    bm = bm.reshape(b, T, SSM_GROUPS, D_STATE)
    cm = cm.reshape(b, T, SSM_GROUPS, D_STATE)
    dt = jax.nn.softplus(dtr.astype(f32) + prm['dt_bias'].astype(f32))
    A = -jnp.exp(prm['a_log'].astype(f32))
    y, h_new = ssd(xs, dt, A, bm, cm, h0)
    y = y + prm['d_skip'].astype(f32)[:, None] * xs.astype(f32)
    y = y.reshape(b, T, SSM_WIDTH) * jax.nn.silu(z.astype(f32))
    yg = y.reshape(b, T, SSM_GROUPS, SSM_WIDTH // SSM_GROUPS)
    yg = yg * lax.rsqrt(jnp.mean(yg * yg, axis=-1, keepdims=True) + EPS)
    y = yg.reshape(b, T, SSM_WIDTH) * prm['ssm_norm'].astype(f32)
    return y.astype(z.dtype), conv_new, h_new


def nsa_mixer(q, q_pos, full_cmp, full_sel, o_win, gates, prm, slopes, sel_qb):
    b, T = q.shape[:2]
    kc = compress(full_cmp[:, :, 0], prm['cmp_pe_k'], prm['cmp_w1_k'], prm['cmp_w2_k'])
    vc = compress(full_cmp[:, :, 1], prm['cmp_pe_v'], prm['cmp_w1_v'], prm['cmp_w2_v'])
    o_cmp, p_cmp = cmp_attention(q, q_pos, kc, vc, slopes)
    n_sel_blocks = -(-full_sel.shape[1] // SEL_BLOCK)
    idx = select_blocks(p_cmp, q_pos, n_sel_blocks)
    o_sel = sel_attention(q, q_pos, idx, full_sel[:, :, 0], full_sel[:, :, 1], slopes, sel_qb)
    g = jax.nn.sigmoid(gates.astype(jnp.float32)).reshape(b, T, 3, ATT_HEADS, 1).astype(q.dtype)
    o = g[:, :, 0] * o_cmp + g[:, :, 1] * o_sel + g[:, :, 2] * o_win
    return o.reshape(b, T, ATT_WIDTH)


def layer(x, q_pos, past, prm, is_prompt):
    x = ffn_half(x, prm['norm_ffn1'], prm['ffn1_wg'], prm['ffn1_wu'], prm['ffn1_wd'])
    h = rmsnorm(x, prm['norm_mix'])
    q, kc, vc, ks, vs, kw, vw, gates, z, xbc, dtr = split_proj(h @ prm['w_in'])
    b, T = x.shape[:2]
    q = q.reshape(b, T, ATT_HEADS, HEAD_DIM)

    def kv(a, c):
        return jnp.stack([a.reshape(b, T, KV_HEADS, HEAD_DIM), c.reshape(b, T, KV_HEADS, HEAD_DIM)], axis=2)

    new_cmp, new_sel, new_win_rows = kv(kc, vc), kv(ks, vs), kv(kw, vw)
    slopes = alibi_slopes()
    if is_prompt:
        full_cmp, full_sel, win_all = new_cmp, new_sel, new_win_rows
        o_win = window_attention_prompt(q, new_win_rows[:, :, 0], new_win_rows[:, :, 1], slopes)
        conv0 = jnp.zeros((b, CONV_W - 1, CONV_DIM), xbc.dtype)
        h0 = jnp.zeros((b, SSM_HEADS, SSM_HEADDIM, D_STATE), jnp.float32)
        sel_qb = min(SEL_QB, T)
    else:
        past_cmp, past_sel, past_win, conv0, h0 = past
        full_cmp = jnp.concatenate([past_cmp, new_cmp], axis=1)
        full_sel = jnp.concatenate([past_sel, new_sel], axis=1)
        win_all = jnp.concatenate([past_win, new_win_rows], axis=1)
        k_pos = q_pos[0] - past_win.shape[1] + jnp.arange(win_all.shape[1])
        o_win = window_chunk(q, q_pos, win_all[:, :, 0], win_all[:, :, 1], k_pos, slopes)
        sel_qb = T
    new_win = win_all[:, win_all.shape[1] - min(WINDOW, win_all.shape[1]):]
    o_att = nsa_mixer(q, q_pos, full_cmp, full_sel, o_win, gates, prm, slopes, sel_qb)
    y_ssm, conv_new, h_new = mamba_mixer(z, xbc, dtr, conv0, h0, prm)
    x = x + jnp.concatenate([o_att, y_ssm], axis=-1) @ prm['w_out']
    x = ffn_half(x, prm['norm_ffn2'], prm['ffn2_wg'], prm['ffn2_wu'], prm['ffn2_wd'])
    return x, (new_cmp, new_sel, new_win, conv_new, h_new)


def setup_inputs(seed: int = 0) -> dict:
    key = jax.random.key(seed)
    ks = jax.random.split(key, 40)
    n_pages = PAST_LEN // PAGE_SIZE
    n_used = DEC_BATCH * n_pages
    n_phys = (5 * n_used + 3) // 4
    win_len = min(WINDOW, PAST_LEN)

    def nrm(k, shape, scale):
        return scale * jax.random.normal(k, shape, jnp.float32)

    def gain(k, n):
        return 1.0 + nrm(k, (DEPTH, n), 0.02)

    dt0 = jnp.exp(jax.random.uniform(ks[30], (DEPTH, SSM_HEADS), jnp.float32, math.log(1e-3), math.log(1e-1)))
    return {
        'x_prompt': nrm(ks[0], (BATCH, SEQ, D_MODEL), 1.0),
        'x_sample': nrm(ks[1], (DEC_BATCH, DEC_SEQ, D_MODEL), 1.0),
        'cache_cmp': nrm(ks[2], (DEPTH, n_phys, PAGE_SIZE, 2, KV_HEADS, HEAD_DIM), 1.0),
        'cache_sel': nrm(ks[3], (DEPTH, n_phys, PAGE_SIZE, 2, KV_HEADS, HEAD_DIM), 1.0),
        'cache_win': nrm(ks[4], (DEPTH, DEC_BATCH, win_len, 2, KV_HEADS, HEAD_DIM), 1.0),
        'state_conv': nrm(ks[5], (DEPTH, DEC_BATCH, CONV_W - 1, CONV_DIM), 1.0),
        'state_ssm': nrm(ks[6], (DEPTH, DEC_BATCH, SSM_HEADS, SSM_HEADDIM, D_STATE), 0.1),
        'page_table': jax.random.permutation(ks[7], n_phys)[:n_used].reshape(DEC_BATCH, n_pages).astype(jnp.int32),
        'norm_ffn1': gain(ks[8], D_MODEL),
        'ffn1_wg': nrm(ks[9], (DEPTH, D_MODEL, D_FF), D_MODEL ** -0.5),
        'ffn1_wu': nrm(ks[10], (DEPTH, D_MODEL, D_FF), D_MODEL ** -0.5),
        'ffn1_wd': nrm(ks[11], (DEPTH, D_FF, D_MODEL), D_FF ** -0.5),
        'norm_mix': gain(ks[12], D_MODEL),
        'w_in': nrm(ks[13], (DEPTH, D_MODEL, IN_W), D_MODEL ** -0.5),
        'cmp_pe_k': nrm(ks[14], (DEPTH, CMP_BLOCK, HEAD_DIM), 0.1),
        'cmp_w1_k': nrm(ks[15], (DEPTH, CMP_BLOCK * HEAD_DIM, CMP_HIDDEN), (CMP_BLOCK * HEAD_DIM) ** -0.5),
        'cmp_w2_k': nrm(ks[16], (DEPTH, CMP_HIDDEN, HEAD_DIM), CMP_HIDDEN ** -0.5),
        'cmp_pe_v': nrm(ks[17], (DEPTH, CMP_BLOCK, HEAD_DIM), 0.1),
        'cmp_w1_v': nrm(ks[18], (DEPTH, CMP_BLOCK * HEAD_DIM, CMP_HIDDEN), (CMP_BLOCK * HEAD_DIM) ** -0.5),
        'cmp_w2_v': nrm(ks[19], (DEPTH, CMP_HIDDEN, HEAD_DIM), CMP_HIDDEN ** -0.5),
        'conv_w': nrm(ks[20], (DEPTH, CONV_W, CONV_DIM), CONV_W ** -0.5),
        'conv_b': nrm(ks[21], (DEPTH, CONV_DIM), 0.01),
        'dt_bias': dt0 + jnp.log(-jnp.expm1(-dt0)),
        'a_log': jnp.log(jax.random.uniform(ks[22], (DEPTH, SSM_HEADS), jnp.float32, 1.0, 16.0)),
        'd_skip': 1.0 + nrm(ks[23], (DEPTH, SSM_HEADS), 0.1),
        'ssm_norm': gain(ks[24], SSM_WIDTH),
        'w_out': nrm(ks[25], (DEPTH, D_MIX, D_MODEL), D_MIX ** -0.5),
        'norm_ffn2': gain(ks[26], D_MODEL),
        'ffn2_wg': nrm(ks[27], (DEPTH, D_MODEL, D_FF), D_MODEL ** -0.5),
        'ffn2_wu': nrm(ks[28], (DEPTH, D_MODEL, D_FF), D_MODEL ** -0.5),
        'ffn2_wd': nrm(ks[29], (DEPTH, D_FF, D_MODEL), D_FF ** -0.5),
        'norm_final': 1.0 + nrm(ks[31], (D_MODEL,), 0.02),
    }


def reference(x_prompt, x_sample, cache_cmp, cache_sel, cache_win, state_conv, state_ssm, page_table,
              norm_ffn1, ffn1_wg, ffn1_wu, ffn1_wd, norm_mix, w_in,
              cmp_pe_k, cmp_w1_k, cmp_w2_k, cmp_pe_v, cmp_w1_v, cmp_w2_v,
              conv_w, conv_b, dt_bias, a_log, d_skip, ssm_norm, w_out,
              norm_ffn2, ffn2_wg, ffn2_wu, ffn2_wd, norm_final):
    weights = dict(norm_ffn1=norm_ffn1, ffn1_wg=ffn1_wg, ffn1_wu=ffn1_wu, ffn1_wd=ffn1_wd,
                   norm_mix=norm_mix, w_in=w_in,
                   cmp_pe_k=cmp_pe_k, cmp_w1_k=cmp_w1_k, cmp_w2_k=cmp_w2_k,
                   cmp_pe_v=cmp_pe_v, cmp_w1_v=cmp_w1_v, cmp_w2_v=cmp_w2_v,
                   conv_w=conv_w, conv_b=conv_b, dt_bias=dt_bias, a_log=a_log, d_skip=d_skip,
                   ssm_norm=ssm_norm, w_out=w_out,
                   norm_ffn2=norm_ffn2, ffn2_wg=ffn2_wg, ffn2_wu=ffn2_wu, ffn2_wd=ffn2_wd)
    n_seq_s, n_pages = page_table.shape
    past_len = n_pages * PAGE_SIZE
    pos_p = jnp.arange(x_prompt.shape[1], dtype=jnp.int32)
    pos_s = past_len + jnp.arange(x_sample.shape[1], dtype=jnp.int32)
    xp, xs = x_prompt, x_sample
    st_p, st_s = [], []
    for l in range(DEPTH):
        prm = {name: w[l] for name, w in weights.items()}
        xp, sp = layer(xp, pos_p, None, prm, True)
        past_cmp = cache_cmp[l][page_table].reshape((n_seq_s, past_len) + cache_cmp.shape[3:])
        past_sel = cache_sel[l][page_table].reshape((n_seq_s, past_len) + cache_sel.shape[3:])
        past = (past_cmp, past_sel, cache_win[l], state_conv[l], state_ssm[l])
        xs, ss = layer(xs, pos_s, past, prm, False)
        st_p.append(sp)
        st_s.append(ss)

    def stacked(lst, i):
        return jnp.stack([s[i] for s in lst], axis=0)

    y_prompt = rmsnorm(xp, norm_final)
    y_sample = rmsnorm(xs, norm_final)
    return (y_prompt, y_sample,
            stacked(st_p, 0), stacked(st_p, 1), stacked(st_p, 2), stacked(st_p, 3), stacked(st_p, 4),
            stacked(st_s, 0), stacked(st_s, 1), stacked(st_s, 2), stacked(st_s, 3), stacked(st_s, 4))
```

```python
import functools

import numpy as np
import jax
import jax.numpy as jnp
from jax import lax
from jax.experimental import pallas as pl
from jax.experimental.pallas import tpu as pltpu

D_MODEL = 1024
ATT_HEADS = 8
HEAD_DIM = 64
ATT_WIDTH = ATT_HEADS * HEAD_DIM
KV_HEADS = 2
HPG = ATT_HEADS // KV_HEADS
KV_W = KV_HEADS * HEAD_DIM
KV_ROW = 2 * KV_W
GATE_W = 3 * ATT_HEADS
CMP_BLOCK = 32
CMP_STRIDE = 16
CMP_HIDDEN = 128
SEL_BLOCK = 64
N_SEL = 16
WINDOW = 512
SSM_WIDTH = 512
SSM_HEADDIM = 64
SSM_HEADS = 8
SSM_GROUPS = 2
D_STATE = 128
CONV_W = 4
CONV_DIM = SSM_WIDTH + 2 * SSM_GROUPS * D_STATE
SSD_CHUNK = 128
D_FF = 2816
EPS = 1e-6
NEG = -1e30
FORCE = 1e9

LANE = 128
SUBLANE = 8
GD_W = LANE
DT_LANE = GATE_W
U_Q, U_CMP, U_SEL, U_WIN = 0, 512, 768, 1024
U_Z, U_XBC, U_GD, U_W = 1280, 1792, 2816, 2944
VMEM_LIMIT = 56 * 1024 * 1024

BF = jnp.bfloat16
F32 = jnp.float32


def _dot(a, b):
    return jnp.dot(a, b, preferred_element_type=F32)


def _dot_nt(a, b):
    return lax.dot_general(a, b, (((1,), (1,)), ((), ())), preferred_element_type=F32)


def _split3(x):
    hi = x.astype(BF)
    r1 = x - hi.astype(F32)
    mid = r1.astype(BF)
    lo = (r1 - mid.astype(F32)).astype(BF)
    return hi, mid, lo


def _bf_rows(x):
    r = x.shape[0]
    pad = -r % (2 * SUBLANE)
    if pad:
        x = jnp.concatenate([x, jnp.zeros((pad, x.shape[1]), x.dtype)], axis=0)
    return x.astype(BF)


def _sigmoid(x):
    return 1.0 / (1.0 + jnp.exp(-x))


def _rms(x, w):
    return x * lax.rsqrt(jnp.mean(x * x, axis=-1, keepdims=True) + EPS) * w


def _cparams(sem):
    return pltpu.CompilerParams(dimension_semantics=sem, vmem_limit_bytes=VMEM_LIMIT)


def _ffn_kernel(*refs, has_mix, has_final):
    it = iter(refs)
    x_ref = next(it)
    if has_mix:
        oa_ref, ys_ref, woa_ref, wob_ref = next(it), next(it), next(it), next(it)
    nw_ref, wg_ref, wu_ref, wd_ref = next(it), next(it), next(it), next(it)
    nf_ref = next(it) if has_final else None
    o_ref = next(it)
    h_ref, acc_ref, xres_ref = next(it), next(it), next(it)

    x = x_ref[...]
    if has_mix:
        x = x + _dot(oa_ref[...].astype(BF), woa_ref[...]) + _dot(ys_ref[...].astype(BF), wob_ref[...])
    xres_ref[...] = x
    h_ref[...] = _rms(x, nw_ref[...]).astype(BF)
    acc_ref[...] = jnp.zeros_like(acc_ref)

    def body(j, carry):
        h = h_ref[...]
        g = _dot(h, wg_ref[j])
        u = _dot(h, wu_ref[j])
        a = (g * _sigmoid(g) * u).astype(BF)
        acc_ref[...] += _dot(a, wd_ref[j])
        return carry

    lax.fori_loop(0, wg_ref.shape[0], body, 0)
    y = xres_ref[...] + 0.5 * acc_ref[...]
    if has_final:
        y = _rms(y, nf_ref[...])
    o_ref[...] = y


def _ffn(x, nw, wg, wu, wd, mix=None, nf=None, tm=512):
    n = x.shape[0]
    tm = min(tm, n)
    assert n % tm == 0
    has_mix, has_final = mix is not None, nf is not None
    row = lambda w: pl.BlockSpec((tm, w), lambda i: (i, 0))
    const = lambda a: pl.BlockSpec(a.shape, lambda i: (0,) * a.ndim, pipeline_mode=pl.Buffered(1))
    args, specs = [x], [row(D_MODEL)]
    if has_mix:
        oa, ys, woa, wob = mix
        args += [oa, ys, woa, wob]
        specs += [row(ATT_WIDTH), row(SSM_WIDTH), const(woa), const(wob)]
    args += [nw, wg, wu, wd]
    specs += [const(nw), const(wg), const(wu), const(wd)]
    if has_final:
        args.append(nf)
        specs.append(const(nf))
    return pl.pallas_call(
        functools.partial(_ffn_kernel, has_mix=has_mix, has_final=has_final),
        grid=(n // tm,),
        in_specs=specs,
        out_specs=row(D_MODEL),
        out_shape=jax.ShapeDtypeStruct((n, D_MODEL), F32),
        scratch_shapes=[pltpu.VMEM((tm, D_MODEL), BF), pltpu.VMEM((tm, D_MODEL), F32),
                        pltpu.VMEM((tm, D_MODEL), F32)],
        compiler_params=_cparams(("arbitrary",)),
        name="ffn_mix_final" if has_mix else "ffn",
    )(*args)


def _proj_kernel(x_ref, nw_ref, w_ref, q_ref, cmp_ref, sel_ref, win_ref, selb_ref, winb_ref, z_ref, xbc_ref, gd_ref):
    h = _rms(x_ref[...], nw_ref[...]).astype(BF)

    def part(a, b):
        return _dot(h, w_ref[:, a:b])

    q_ref[...] = part(U_Q, U_CMP)
    cmp_ref[...] = part(U_CMP, U_SEL)
    s = part(U_SEL, U_WIN)
    sel_ref[...] = s
    selb_ref[...] = s.astype(BF)
    w = part(U_WIN, U_Z)
    win_ref[...] = w
    winb_ref[...] = w.astype(BF)
    z_ref[...] = part(U_Z, U_XBC)
    xbc_ref[...] = part(U_XBC, U_GD)
    gd_ref[...] = part(U_GD, U_W)


def _proj(x, nw, w, tm=512):
    n = x.shape[0]
    tm = min(tm, n)
    assert n % tm == 0
    row = lambda c: pl.BlockSpec((tm, c), lambda i: (i, 0))
    const = lambda a: pl.BlockSpec(a.shape, lambda i: (0,) * a.ndim, pipeline_mode=pl.Buffered(1))
    widths = [(ATT_WIDTH, F32), (KV_ROW, F32), (KV_ROW, F32), (KV_ROW, F32), (KV_ROW, BF), (KV_ROW, BF),
              (SSM_WIDTH, F32), (CONV_DIM, F32), (GD_W, F32)]
    return pl.pallas_call(
        _proj_kernel,
        grid=(n // tm,),
        in_specs=[row(D_MODEL), const(nw), const(w)],
        out_specs=[row(c) for c, _ in widths],
        out_shape=[jax.ShapeDtypeStruct((n, c), d) for c, d in widths],
        compiler_params=_cparams(("arbitrary",)),
        name="proj",
    )(x, nw, w)


def _compress_kernel(rows_ref, pe_ref, w1_ref, w2_ref, kc_ref, vc_ref, *, nchunks):
    keep = lax.broadcasted_iota(jnp.int32, (nchunks, KV_W), 0) < nchunks - 1
    for kv, o_ref in ((0, kc_ref), (1, vc_ref)):
        a = jnp.zeros((nchunks, KV_HEADS * CMP_HIDDEN), F32)
        b = jnp.zeros((nchunks, KV_HEADS * CMP_HIDDEN), F32)
        for r in range(CMP_STRIDE):
            x = rows_ref[0, pl.ds(2 * r + kv, nchunks, stride=2 * CMP_STRIDE), :]
            a = a + _dot((x + pe_ref[kv, r:r + 1, :]).astype(BF), w1_ref[kv, r])
            rb = CMP_STRIDE + r
            b = b + _dot((x + pe_ref[kv, rb:rb + 1, :]).astype(BF), w1_ref[kv, rb])
        hid = a + pltpu.roll(b, nchunks - 1, 0)
        act = (hid * _sigmoid(hid)).astype(BF)
        out = _dot(act, w2_ref[kv])
        o_ref[0] = jnp.where(keep, out, 0.0).astype(BF)


def _compress(rows, pe, w1, w2):
    b, t, _ = rows.shape
    nchunks = t // CMP_STRIDE
    const = lambda a: pl.BlockSpec(a.shape, lambda i: (0,) * a.ndim, pipeline_mode=pl.Buffered(1))
    out = pl.BlockSpec((1, nchunks, KV_W), lambda i: (i, 0, 0))
    return pl.pallas_call(
        functools.partial(_compress_kernel, nchunks=nchunks),
        grid=(b,),
        in_specs=[pl.BlockSpec((1, 2 * nchunks * CMP_STRIDE, KV_W), lambda i: (i, 0, 0)),
                  const(pe), const(w1), const(w2)],
        out_specs=[out, out],
        out_shape=[jax.ShapeDtypeStruct((b, nchunks, KV_W), BF)] * 2,
        compiler_params=_cparams(("arbitrary",)),
        name="compress",
    )(rows.reshape(b, 2 * t, KV_W), pe, w1, w2)


def _attn_kernel(q_ref, gd_ref, kc_ref, vc_ref, sel_ref, win_ref, o_ref, m_ref, l_ref, acc_ref, *,
                 tq, p0, w0, nb, nsel, tk, tw):
    nbp = kc_ref.shape[1]
    nselp = -(-nsel // LANE) * LANE
    t0 = pl.program_id(1) * tq
    base = p0 + t0
    lane = lax.broadcasted_iota(jnp.int32, (tq, LANE), 1)
    is_g0 = lane < HEAD_DIM
    qf = q_ref[0]

    def stacked_q(g):
        keep = is_g0 if g == 0 else jnp.logical_not(is_g0)
        parts = [jnp.where(keep, qf[:, LANE * k:LANE * (k + 1)], 0.0) for k in range(HPG)]
        return jnp.concatenate(parts, axis=0).astype(BF)

    qg = [stacked_q(g) for g in range(KV_HEADS)]

    def slope(g, k):
        return 2.0 ** (-(HPG * g + k + 1))

    kc = kc_ref[0]
    vc = vc_ref[0]
    col = lax.broadcasted_iota(jnp.int32, (tq, nbp), 1)
    tqn = base + lax.broadcasted_iota(jnp.int32, (tq, nbp), 0)
    dist = tqn - (col * CMP_STRIDE + (CMP_BLOCK - 1))
    cmask = jnp.where(col < nb, dist, -1) >= 0
    distf = dist.astype(F32)
    ii = lax.broadcasted_iota(jnp.int32, (nbp, nselp), 0)
    jj = lax.broadcasted_iota(jnp.int32, (nbp, nselp), 1)
    ratio = SEL_BLOCK // CMP_STRIDE
    rsh = int(np.log2(ratio))
    ov = (jnp.where(jnp.right_shift(ii, rsh) == jj, 1.0, 0.0)
          + jnp.where(jnp.right_shift(ii + 1, rsh) == jj, 1.0, 0.0)).astype(BF)
    js = lax.broadcasted_iota(jnp.int32, (tq, nselp), 1)
    tqs = base + lax.broadcasted_iota(jnp.int32, (tq, nselp), 0)
    ssh = int(np.log2(SEL_BLOCK))
    forced = jnp.where(js == 0, 1, jnp.where(js == jnp.right_shift(tqs, ssh), 1, 0)) > 0
    o_cmp, selm = [], []
    for g in range(KV_HEADS):
        s = _dot_nt(qg[g], kc)
        ps = []
        for k in range(HPG):
            sk = s[k * tq:(k + 1) * tq] - slope(g, k) * distf
            sk = jnp.where(cmask, sk, NEG)
            e = jnp.where(cmask, jnp.exp(sk - jnp.max(sk, axis=-1, keepdims=True)), 0.0)
            ps.append(e / jnp.maximum(jnp.sum(e, axis=-1, keepdims=True), 1e-30))
        p = jnp.concatenate(ps, axis=0)
        o_cmp.append(_dot(p.astype(BF), vc))
        psum = ps[0] + ps[1] + ps[2] + ps[3]
        hi = psum.astype(BF).astype(F32)
        mid = (psum - hi).astype(BF).astype(F32)
        lo = psum - hi - mid
        imp = sum(_dot(_bf_rows(t), ov) for t in (hi, mid, lo))[:tq]
        imp = jnp.where(forced, FORCE, jnp.where(js * SEL_BLOCK <= tqs, imp, -1.0))
        imp = jnp.where(js < nsel, imp, -2.0)
        rank = jnp.zeros((tq, nselp), F32)
        for i in range(nsel):
            ci = imp[:, i:i + 1]
            rank = rank + jnp.where(ci > imp, 1.0, jnp.where(ci == imp, jnp.where(js > i, 1.0, 0.0), 0.0))
        selm.append(_bf_rows(jnp.where(rank < N_SEL, 1.0, 0.0)))

    def init():
        m_ref[...] = jnp.full(m_ref.shape, NEG, F32)
        l_ref[...] = jnp.zeros(l_ref.shape, F32)
        acc_ref[...] = jnp.zeros(acc_ref.shape, F32)

    def flash_step(g, kk, vv, ok, relpos):
        s = _dot_nt(qg[g], kk)
        sks = []
        for k in range(HPG):
            sk = s[k * tq:(k + 1) * tq] + slope(g, k) * relpos
            sks.append(jnp.where(ok, sk, NEG))
        s = jnp.concatenate(sks, axis=0)
        m_prev = m_ref[g]
        m_new = jnp.maximum(m_prev, jnp.max(s, axis=-1, keepdims=True))
        alpha = jnp.exp(m_prev - m_new)
        p = jnp.exp(s - m_new)
        l_ref[g] = alpha * l_ref[g] + jnp.sum(p, axis=-1, keepdims=True)
        acc_ref[g] = alpha * acc_ref[g] + _dot(p.astype(BF), vv)
        m_ref[g] = m_new

    def finish():
        return [acc_ref[g] / jnp.maximum(l_ref[g], 1e-30) for g in range(KV_HEADS)]

    init()
    n_kt = jnp.minimum((base + tq + tk - 1) // tk, sel_ref.shape[1] // tk)

    def sel_body(kt, carry):
        k0 = pl.multiple_of(kt * tk, tk)
        kk = sel_ref[0, pl.ds(k0, tk), 0:KV_W]
        vv = sel_ref[0, pl.ds(k0, tk), KV_W:KV_ROW]
        eb = lax.broadcasted_iota(jnp.int32, (nselp, tk), 0)
        ec = k0 + lax.broadcasted_iota(jnp.int32, (nselp, tk), 1)
        expand = jnp.where(eb == jnp.right_shift(ec, ssh), 1.0, 0.0).astype(BF)
        kp = k0 + lax.broadcasted_iota(jnp.int32, (tq, tk), 1)
        tqk = base + lax.broadcasted_iota(jnp.int32, (tq, tk), 0)
        causal = kp <= tqk
        relpos = (kp[0:1, :] - base).astype(F32)
        for g in range(KV_HEADS):
            picked = _dot(selm[g], expand)[:tq]
            ok = jnp.where(causal, picked, 0.0) > 0.5
            flash_step(g, kk, vv, ok, relpos)
        return carry

    lax.fori_loop(0, n_kt, sel_body, 0)
    o_sel = finish()

    init()
    lo = jnp.maximum(base - (WINDOW - 1) - w0, 0) // tw
    hi = jnp.minimum((base + tq - 1 - w0) // tw + 1, win_ref.shape[1] // tw)

    def win_body(kt, carry):
        k0 = pl.multiple_of(kt * tw, tw)
        kk = win_ref[0, pl.ds(k0, tw), 0:KV_W]
        vv = win_ref[0, pl.ds(k0, tw), KV_W:KV_ROW]
        kp = w0 + k0 + lax.broadcasted_iota(jnp.int32, (tq, tw), 1)
        tqk = base + lax.broadcasted_iota(jnp.int32, (tq, tw), 0)
        d = tqk - kp
        ok = jnp.right_shift(d, int(np.log2(WINDOW))) == 0
        relpos = (kp[0:1, :] - base).astype(F32)
        for g in range(KV_HEADS):
            flash_step(g, kk, vv, ok, relpos)
        return carry

    lax.fori_loop(lo, hi, win_body, 0)
    o_win = finish()

    sg = _sigmoid(gd_ref[0])
    for k in range(HPG):
        def slab(o):
            return jnp.where(is_g0, o[0][k * tq:(k + 1) * tq], o[1][k * tq:(k + 1) * tq])

        def gate(br):
            c0 = br * ATT_HEADS + k
            c1 = c0 + HPG
            return jnp.where(is_g0, sg[:, c0:c0 + 1], sg[:, c1:c1 + 1])

        o_ref[0, :, LANE * k:LANE * (k + 1)] = gate(0) * slab(o_cmp) + gate(1) * slab(o_sel) + gate(2) * slab(o_win)


def _attn(q, gd, kc, vc, selb, winb, *, tq, p0, w0, nb, nsel, tk, tw):
    b, t, _ = q.shape
    assert t % tq == 0 and selb.shape[1] % tk == 0 and winb.shape[1] % tw == 0
    full = lambda a: pl.BlockSpec((1,) + a.shape[1:], lambda i, j: (i, 0, 0))
    return pl.pallas_call(
        functools.partial(_attn_kernel, tq=tq, p0=p0, w0=w0, nb=nb, nsel=nsel, tk=tk, tw=tw),
        grid=(b, t // tq),
        in_specs=[pl.BlockSpec((1, tq, ATT_WIDTH), lambda i, j: (i, j, 0)),
                  pl.BlockSpec((1, tq, GD_W), lambda i, j: (i, j, 0)),
                  full(kc), full(vc), full(selb), full(winb)],
        out_specs=pl.BlockSpec((1, tq, ATT_WIDTH), lambda i, j: (i, j, 0)),
        out_shape=jax.ShapeDtypeStruct((b, t, ATT_WIDTH), F32),
        scratch_shapes=[pltpu.VMEM((KV_HEADS, HPG * tq, 1), F32), pltpu.VMEM((KV_HEADS, HPG * tq, 1), F32),
                        pltpu.VMEM((KV_HEADS, HPG * tq, LANE), F32)],
        compiler_params=_cparams(("arbitrary", "arbitrary")),
        name="nsa_attn",
    )(q, gd, kc, vc, selb, winb)


def _mamba_kernel(z_ref, xbc_ref, gd_ref, conv0_ref, h0_ref, cw_ref, cb_ref, dtb_ref, arow_ref, dskip_ref,
                  nrm_ref, y_ref, conv_ref, hout_ref, xbuf, hst, zbuf, dbuf, *, lv, L):
    c = pl.program_id(1)
    pad = SUBLANE

    @pl.when(c == 0)
    def _():
        xbuf[...] = jnp.zeros(xbuf.shape, F32)
        xbuf[pad - (CONV_W - 1):pad, :] = conv0_ref[0]
        hst[...] = h0_ref[0].reshape(hst.shape)
        if lv < L:
            zbuf[...] = jnp.zeros(zbuf.shape, F32)
            dbuf[...] = jnp.zeros(dbuf.shape, F32)

    xbuf[pad:pad + lv, :] = xbc_ref[0]
    if lv < L:
        zbuf[0:lv, :] = z_ref[0]
        dbuf[0:lv, :] = gd_ref[0]
        z, dtr = zbuf[...], dbuf[...]
    else:
        z, dtr = z_ref[0], gd_ref[0]

    conv = cb_ref[...]
    for tap in range(CONV_W):
        o = pad - (CONV_W - 1) + tap
        conv = conv + xbuf[o:o + L, :] * cw_ref[tap:tap + 1, :]
    carry = xbuf[lv + pad - (CONV_W - 1):lv + pad, :]
    conv_ref[0] = carry
    xbuf[pad - (CONV_W - 1):pad, :] = carry
    xc = conv * _sigmoid(conv)
    xs = xc[:, :SSM_WIDTH]
    bm = [xc[:, SSM_WIDTH + D_STATE * g:SSM_WIDTH + D_STATE * (g + 1)] for g in range(SSM_GROUPS)]
    cm = [xc[:, SSM_WIDTH + D_STATE * (SSM_GROUPS + g):SSM_WIDTH + D_STATE * (SSM_GROUPS + g + 1)].astype(BF)
          for g in range(SSM_GROUPS)]

    lane = lax.broadcasted_iota(jnp.int32, (L, LANE), 1)
    rowi = lax.broadcasted_iota(jnp.int32, (L, LANE), 0)
    xdt_in = dtr + dtb_ref[...]
    dt = jnp.maximum(xdt_in, 0.0) + jnp.log1p(jnp.exp(-jnp.abs(xdt_in)))
    dt = jnp.where(lane >= DT_LANE, jnp.where(lane < DT_LANE + SSM_HEADS, dt, 0.0), 0.0)
    if lv < L:
        dt = jnp.where(rowi < lv, dt, 0.0)
    da = dt * arow_ref[...]
    tri = (lax.broadcasted_iota(jnp.int32, (L, L), 1) <= lax.broadcasted_iota(jnp.int32, (L, L), 0))
    trib = jnp.where(tri, 1.0, 0.0).astype(BF)
    acs = sum(_dot(trib, t) for t in _split3(da))
    acs_t = acs.T
    ea = jnp.exp(acs)
    de = jnp.exp(acs[L - 1:L, :] - acs)
    is_lo = lane < SSM_HEADDIM
    sub_lo = rowi < SSM_HEADDIM

    def per_pair(col_of, k):
        a, b = DT_LANE + 2 * k, DT_LANE + 2 * k + 1
        return jnp.where(is_lo, col_of[:, a:a + 1], col_of[:, b:b + 1])

    cb = [_dot_nt(cm[g], bm[g].astype(BF)) for g in range(SSM_GROUPS)]
    ys = []
    for k in range(SSM_HEADS // 2):
        g = (2 * k) // (SSM_HEADS // SSM_GROUPS)
        xs_p = xs[:, LANE * k:LANE * (k + 1)]
        xdt = xs_p * per_pair(dt, k)
        xdt_b = xdt.astype(BF)
        yd = []
        for h in (2 * k, 2 * k + 1):
            seg = acs[:, DT_LANE + h:DT_LANE + h + 1] - acs_t[DT_LANE + h:DT_LANE + h + 1, :]
            decay = jnp.exp(jnp.where(tri, seg, NEG))
            yd.append(_dot((cb[g] * decay).astype(BF), xdt_b))
        y_diag = jnp.where(is_lo, yd[0], yd[1])
        h_prev = hst[LANE * k:LANE * (k + 1), :]
        y_off = _dot_nt(cm[g], h_prev.astype(BF)) * per_pair(ea, k)
        ys.append(y_diag + y_off + dskip_ref[:, LANE * k:LANE * (k + 1)] * xs_p)
        st = _dot((xdt * per_pair(de, k)).T.astype(BF), bm[g].astype(BF))
        cd_a = jnp.broadcast_to(acs_t[DT_LANE + 2 * k:DT_LANE + 2 * k + 1, L - 1:L], (L, LANE))
        cd_b = jnp.broadcast_to(acs_t[DT_LANE + 2 * k + 1:DT_LANE + 2 * k + 2, L - 1:L], (L, LANE))
        hst[LANE * k:LANE * (k + 1), :] = h_prev * jnp.exp(jnp.where(sub_lo, cd_a, cd_b)) + st
    y = jnp.concatenate(ys, axis=1) * (z * _sigmoid(z))
    gw = SSM_WIDTH // SSM_GROUPS
    yn = []
    for g in range(SSM_GROUPS):
        yg = y[:, gw * g:gw * (g + 1)]
        yn.append(yg * lax.rsqrt(jnp.mean(yg * yg, axis=-1, keepdims=True) + EPS))
    y = jnp.concatenate(yn, axis=1) * nrm_ref[...]
    y_ref[0] = y[0:lv]
    hout_ref[0] = hst[...].reshape(hout_ref.shape[1:])


def _mamba(z, xbc, gd, conv0, h0, cw, cb, dtb, arow, dskip, nrm):
    b, t, _ = z.shape
    L = SSD_CHUNK
    lv = L if t % L == 0 else t
    assert lv == L or (t < L and t % SUBLANE == 0)
    nc = t // lv
    const = lambda a: pl.BlockSpec(a.shape, lambda i, j: (0,) * a.ndim)
    tok = lambda w: pl.BlockSpec((1, lv, w), lambda i, j: (i, j, 0))
    per_b = lambda a: pl.BlockSpec((1,) + a.shape[1:], lambda i, j: (i,) + (0,) * (a.ndim - 1))
    return pl.pallas_call(
        functools.partial(_mamba_kernel, lv=lv, L=L),
        grid=(b, nc),
        in_specs=[tok(SSM_WIDTH), tok(CONV_DIM), tok(GD_W), per_b(conv0), per_b(h0),
                  const(cw), const(cb), const(dtb), const(arow), const(dskip), const(nrm)],
        out_specs=[tok(SSM_WIDTH), per_b(conv0), per_b(h0)],
        out_shape=[jax.ShapeDtypeStruct((b, t, SSM_WIDTH), F32), jax.ShapeDtypeStruct(conv0.shape, F32),
                   jax.ShapeDtypeStruct(h0.shape, F32)],
        scratch_shapes=[pltpu.VMEM((L + 2 * SUBLANE, CONV_DIM), F32), pltpu.VMEM((SSM_HEADS * SSM_HEADDIM, D_STATE), F32),
                        pltpu.VMEM((L, SSM_WIDTH), F32), pltpu.VMEM((L, GD_W), F32)],
        compiler_params=_cparams(("arbitrary", "arbitrary")),
        name="mamba",
    )(z, xbc, gd, conv0, h0, cw, cb, dtb, arow, dskip, nrm)


def _gather_kernel(pt_ref, *refs, ppg, n_pg, with_tail):
    pages = refs[:ppg]
    if with_tail:
        new_ref, o_ref = refs[ppg], refs[ppg + 1]
    else:
        o_ref = refs[ppg]
    pg = pl.program_id(1)
    psz = pages[0].shape[1]

    @pl.when(pg < n_pg)
    def _():
        for k in range(ppg):
            o_ref[0, psz * k:psz * (k + 1), :] = pages[k][0].astype(o_ref.dtype)

    if with_tail:
        @pl.when(pg == n_pg)
        def _():
            new = new_ref[0]
            fill = jnp.zeros((o_ref.shape[1] - new.shape[0], new.shape[1]), new.dtype)
            o_ref[0] = jnp.concatenate([new, fill], axis=0).astype(o_ref.dtype)


def _gather_pages(cache, page_table, new_rows=None, out_dtype=F32, ppg=8):
    s, n_pages = page_table.shape
    psz = cache.shape[1]
    ppg = min(ppg, n_pages)
    assert n_pages % ppg == 0
    n_pg = n_pages // ppg
    with_tail = new_rows is not None

    def page_spec(k):
        return pl.BlockSpec((1, psz, KV_ROW), lambda i, j, pt: (pt[i, jnp.minimum(j * ppg + k, n_pages - 1)], 0, 0))

    in_specs = [page_spec(k) for k in range(ppg)]
    args = [cache] * ppg
    if with_tail:
        in_specs.append(pl.BlockSpec((1,) + new_rows.shape[1:], lambda i, j, pt: (i, 0, 0)))
        args.append(new_rows)
    n_blk = n_pg + (1 if with_tail else 0)
    return pl.pallas_call(
        functools.partial(_gather_kernel, ppg=ppg, n_pg=n_pg, with_tail=with_tail),
        grid_spec=pltpu.PrefetchScalarGridSpec(
            num_scalar_prefetch=1,
            grid=(s, n_blk),
            in_specs=in_specs,
            out_specs=pl.BlockSpec((1, ppg * psz, KV_ROW), lambda i, j, pt: (i, j, 0)),
        ),
        out_shape=jax.ShapeDtypeStruct((s, n_blk * ppg * psz, KV_ROW), out_dtype),
        compiler_params=_cparams(("arbitrary", "arbitrary")),
        name="gather_sel" if with_tail else "gather_cmp",
    )(page_table, *args)


def _head_perm():
    cols = []
    for k in range(HPG):
        cols += list(range(k * HEAD_DIM, (k + 1) * HEAD_DIM))
        cols += list(range((HPG + k) * HEAD_DIM, (HPG + k + 1) * HEAD_DIM))
    return np.asarray(cols)


def _prep_layer(prm):
    hp = _head_perm()
    w_in = prm['w_in']
    o_cmp, o_gate = ATT_WIDTH, ATT_WIDTH + 6 * KV_W
    o_z = o_gate + GATE_W
    o_xbc = o_z + SSM_WIDTH
    o_dt = o_xbc + CONV_DIM
    w_p = jnp.concatenate([
        w_in[:, hp] * (HEAD_DIM ** -0.5),
        w_in[:, o_cmp:o_gate], w_in[:, o_z:o_xbc], w_in[:, o_xbc:o_dt],
        w_in[:, o_gate:o_z], w_in[:, o_dt:o_dt + SSM_HEADS],
        jnp.zeros((D_MODEL, GD_W - GATE_W - SSM_HEADS), F32)], axis=1).astype(BF)
    assert w_p.shape[1] == U_W

    def blockdiag(blocks):
        z = jnp.zeros_like(blocks[0])
        rows = [jnp.concatenate([blk if j == i else z for j in range(len(blocks))], axis=-1)
                for i, blk in enumerate(blocks)]
        return jnp.concatenate(rows, axis=-2)

    w1k = prm['cmp_w1_k'].reshape(CMP_BLOCK, HEAD_DIM, CMP_HIDDEN)
    w1v = prm['cmp_w1_v'].reshape(CMP_BLOCK, HEAD_DIM, CMP_HIDDEN)
    w1 = jnp.stack([blockdiag([w1k, w1k]), blockdiag([w1v, w1v])]).astype(BF)
    w2 = jnp.stack([blockdiag([prm['cmp_w2_k']] * 2), blockdiag([prm['cmp_w2_v']] * 2)]).astype(BF)
    pe = jnp.stack([jnp.concatenate([prm['cmp_pe_k']] * 2, axis=1), jnp.concatenate([prm['cmp_pe_v']] * 2, axis=1)])

    def ffn_tiles(wg, wu, wd, tf=256):
        nt = D_FF // tf
        cols = lambda a: a.reshape(D_MODEL, nt, tf).transpose(1, 0, 2).astype(BF)
        return cols(wg), cols(wu), wd.reshape(nt, tf, D_MODEL).astype(BF)

    def dt_lanes(v):
        return jnp.zeros((1, GD_W), F32).at[0, DT_LANE:DT_LANE + SSM_HEADS].set(v)

    w_out = prm['w_out']
    row2 = lambda v: v.reshape(1, -1)
    return dict(
        w_in=w_p, w1=w1, w2=w2, pe=pe,
        conv_w=prm['conv_w'], conv_b=row2(prm['conv_b']),
        dtb=dt_lanes(prm['dt_bias']), arow=dt_lanes(-jnp.exp(prm['a_log'])),
        dskip=row2(jnp.repeat(prm['d_skip'], SSM_HEADDIM)), ssm_norm=row2(prm['ssm_norm']),
        wo_a=w_out[:ATT_WIDTH][hp].astype(BF), wo_b=w_out[ATT_WIDTH:].astype(BF),
        norm_ffn1=row2(prm['norm_ffn1']), norm_mix=row2(prm['norm_mix']), norm_ffn2=row2(prm['norm_ffn2']),
        ffn1=ffn_tiles(prm['ffn1_wg'], prm['ffn1_wu'], prm['ffn1_wd']),
        ffn2=ffn_tiles(prm['ffn2_wg'], prm['ffn2_wu'], prm['ffn2_wd']),
    )


def _layer(x, w, nf, past, page_table):
    b, t, _ = x.shape
    n = b * t
    x1 = _ffn(x.reshape(n, D_MODEL), w['norm_ffn1'], *w['ffn1'])
    q, cmp_r, sel_r, win_r, sel_b, win_b, z, xbc, gd = _proj(x1, w['norm_mix'], w['w_in'])
    r3 = lambda a: a.reshape(b, t, a.shape[-1])
    if past is None:
        p0, w0 = 0, 0
        cmp_rows, sel_rows, win_rows = r3(cmp_r), r3(sel_b), r3(win_b)
        t_cmp, t_sel = t, t
        conv0 = jnp.zeros((b, CONV_W - 1, CONV_DIM), F32)
        h0 = jnp.zeros((b, SSM_HEADS, SSM_HEADDIM, D_STATE), F32)
        tq, tk = min(128, t), min(512, t)
    else:
        cache_cmp, cache_sel, cache_win, conv0, h0 = past
        n_phys, psz = cache_cmp.shape[:2]
        past_len = page_table.shape[1] * psz
        p0, w0 = past_len, past_len - cache_win.shape[1]
        assert past_len % CMP_STRIDE == 0 and t < CMP_STRIDE
        cmp_rows = _gather_pages(cache_cmp.reshape(n_phys, psz, KV_ROW), page_table)
        sel_rows = _gather_pages(cache_sel.reshape(n_phys, psz, KV_ROW), page_table, new_rows=r3(sel_r), out_dtype=BF)
        win_all = jnp.concatenate([cache_win.reshape(b, -1, KV_ROW), r3(win_r)], axis=1)
        n_win = win_all.shape[1]
        new_win = win_all[:, n_win - min(WINDOW, n_win):]
        win_rows = jnp.pad(win_all, ((0, 0), (0, -n_win % LANE), (0, 0))).astype(BF)
        t_cmp, t_sel = past_len + t, past_len + t
        tq, tk = t, min(512, sel_rows.shape[1])
    nb = t_cmp // CMP_STRIDE - 1
    nsel = -(-t_sel // SEL_BLOCK)
    kc, vc = _compress(cmp_rows, w['pe'], w['w1'], w['w2'])
    o_att = _attn(r3(q), r3(gd), kc, vc, sel_rows, win_rows,
                  tq=tq, p0=p0, w0=w0, nb=nb, nsel=nsel, tk=tk, tw=min(LANE, win_rows.shape[1]))
    y_ssm, conv_new, h_new = _mamba(r3(z), r3(xbc), r3(gd), conv0, h0, w['conv_w'], w['conv_b'], w['dtb'],
                                    w['arow'], w['dskip'], w['ssm_norm'])
    x2 = _ffn(x1, w['norm_ffn2'], *w['ffn2'],
              mix=(o_att.reshape(n, ATT_WIDTH), y_ssm.reshape(n, SSM_WIDTH), w['wo_a'], w['wo_b']), nf=nf)
    kv = lambda a: a.reshape(b, t, 2, KV_HEADS, HEAD_DIM)
    if past is None:
        new_win = kv(win_r)[:, t - min(WINDOW, t):]
    else:
        new_win = new_win.reshape(b, -1, 2, KV_HEADS, HEAD_DIM)
    return x2.reshape(b, t, D_MODEL), (kv(cmp_r), kv(sel_r), new_win, conv_new, h_new)


def kernel(x_prompt, x_sample, cache_cmp, cache_sel, cache_win, state_conv, state_ssm, page_table, norm_ffn1, ffn1_wg, ffn1_wu, ffn1_wd, norm_mix, w_in, cmp_pe_k, cmp_w1_k, cmp_w2_k, cmp_pe_v, cmp_w1_v, cmp_w2_v, conv_w, conv_b, dt_bias, a_log, d_skip, ssm_norm, w_out, norm_ffn2, ffn2_wg, ffn2_wu, ffn2_wd, norm_final):
    weights = dict(norm_ffn1=norm_ffn1, ffn1_wg=ffn1_wg, ffn1_wu=ffn1_wu, ffn1_wd=ffn1_wd,
                   norm_mix=norm_mix, w_in=w_in,
                   cmp_pe_k=cmp_pe_k, cmp_w1_k=cmp_w1_k, cmp_w2_k=cmp_w2_k,
                   cmp_pe_v=cmp_pe_v, cmp_w1_v=cmp_w1_v, cmp_w2_v=cmp_w2_v,
                   conv_w=conv_w, conv_b=conv_b, dt_bias=dt_bias, a_log=a_log, d_skip=d_skip,
                   ssm_norm=ssm_norm, w_out=w_out,
                   norm_ffn2=norm_ffn2, ffn2_wg=ffn2_wg, ffn2_wu=ffn2_wu, ffn2_wd=ffn2_wd)
    depth = w_in.shape[0]
    assert depth == 1, "the final norm is fused into the layer's last kernel"
    nf = norm_final.reshape(1, -1)
    xp, xs = x_prompt, x_sample
    st_p, st_s = [], []
    for l in range(depth):
        w = _prep_layer({name: v[l] for name, v in weights.items()})
        xp, sp = _layer(xp, w, nf, None, None)
        xs, ss = _layer(xs, w, nf, (cache_cmp[l], cache_sel[l], cache_win[l], state_conv[l], state_ssm[l]), page_table)
        st_p.append(sp)
        st_s.append(ss)
    stacked = lambda lst, i: jnp.stack([s[i] for s in lst], axis=0)
    return (xp, xs,
            stacked(st_p, 0), stacked(st_p, 1), stacked(st_p, 2), stacked(st_p, 3), stacked(st_p, 4),
            stacked(st_s, 0), stacked(st_s, 1), stacked(st_s, 2), stacked(st_s, 3), stacked(st_s, 4))
```
